```python
import jax, jax.numpy as jnp
from jax import lax
import numpy as np

D_MODEL = 2048
BATCH = 1
SEQ = 8192
DEPTH = 1

CHUNK = 128
A_HEADS = 8
A_HEAD_DIM = D_MODEL // A_HEADS
A_WIDTH = A_HEADS * A_HEAD_DIM
POOL_WINDOWS = (2, 4, 8, 16)
B_GROUPS = len(POOL_WINDOWS)
B_GROUP_DIM = D_MODEL // 8
B_WIDTH = B_GROUPS * B_GROUP_DIM
IN_COLS = 2 * A_WIDTH + B_WIDTH + 2 * D_MODEL
FFN_DIM = 5632
CONV_WIDTH = 3
N_MOD = 6
EPS = 1e-6

kernel_name = "hybrid_sgu_pool_convffn_block"


def rms_norm(x, g):
    xf = x.astype(jnp.float32)
    y = xf * lax.rsqrt(jnp.mean(xf * xf, axis=-1, keepdims=True) + EPS)
    return (y * g.astype(jnp.float32)).astype(x.dtype)


def layer_norm(x, g, b):
    xf = x.astype(jnp.float32)
    mu = jnp.mean(xf, axis=-1, keepdims=True)
    xc = xf - mu
    y = xc * lax.rsqrt(jnp.mean(xc * xc, axis=-1, keepdims=True) + EPS)
    return (y * g.astype(jnp.float32) + b.astype(jnp.float32)).astype(x.dtype)


def spatial_gating(u, v, ln_g, ln_b, w_s, b_s):
    B, S, _ = v.shape
    n_chunks = S // CHUNK
    vn = layer_norm(v, ln_g, ln_b).reshape(B, n_chunks, CHUNK, A_HEADS, A_HEAD_DIM)
    mask = jnp.tril(jnp.ones((CHUNK, CHUNK), dtype=bool))
    w = jnp.where(mask[None], w_s, jnp.zeros_like(w_s)).astype(vn.dtype)
    mixed = jnp.einsum('hij,bnjhd->bnihd', w, vn) + b_s.T[None, None, :, :, None]
    return u * mixed.reshape(B, S, A_WIDTH)


def multiscale_pool(p, w_pool, pool_scale):
    B, S, _ = p.shape
    groups = p.reshape(B, S, B_GROUPS, B_GROUP_DIM)
    pos = jnp.arange(1, S + 1, dtype=jnp.float32)[None, :, None]
    outs = []
    for gi, win in enumerate(POOL_WINDOWS):
        pg = groups[:, :, gi].astype(jnp.float32)
        cs = jnp.cumsum(pg, axis=1)
        lagged = jnp.pad(cs, ((0, 0), (win, 0), (0, 0)))[:, :S]
        mean = (cs - lagged) / jnp.minimum(pos, float(win))
        pooled = (mean - pg).astype(p.dtype)
        outs.append(jnp.einsum('bsc,ce->bse', pooled, w_pool[gi]))
    return jnp.concatenate(outs, axis=-1) * pool_scale


def causal_depthwise_conv(h, w, b):
    S = h.shape[1]
    hp = jnp.pad(h, ((0, 0), (CONV_WIDTH - 1, 0), (0, 0)))
    out = b
    for k in range(CONV_WIDTH):
        out = out + hp[:, k:k + S] * w[k]
    return out


def hybrid_layer(x, c, w_ada, b_ada, g_pre_mix, g_post_mix, w_in, ln_v_g, ln_v_b,
                 w_spatial, b_spatial, w_pool, pool_scale, w_branch_a, w_branch_b,
                 w_out, g_pre_ffn, g_post_ffn, w_up, conv_w, conv_b, w_down):
    mod = jax.nn.silu(c) @ w_ada + b_ada
    shift_m, scale_m, gate_m, shift_f, scale_f, gate_f = [
        m[:, None, :] for m in jnp.split(mod, N_MOD, axis=-1)]

    h = rms_norm(x, g_pre_mix) * (1.0 + scale_m) + shift_m
    proj = h @ w_in
    u, v, p, ga, gb = jnp.split(
        proj, [A_WIDTH, 2 * A_WIDTH, 2 * A_WIDTH + B_WIDTH, 2 * A_WIDTH + B_WIDTH + D_MODEL], axis=-1)
    u = jax.nn.gelu(u)
    v = jax.nn.gelu(v)
    y_a = spatial_gating(u, v, ln_v_g, ln_v_b, w_spatial, b_spatial) @ w_branch_a
    y_b = multiscale_pool(p, w_pool, pool_scale) @ w_branch_b
    merged = jax.nn.sigmoid(ga) * y_a + jax.nn.sigmoid(gb) * y_b
    mix = merged @ w_out
    x = x + gate_m * rms_norm(mix, g_post_mix)

    h = rms_norm(x, g_pre_ffn) * (1.0 + scale_f) + shift_f
    up = causal_depthwise_conv(h @ w_up, conv_w, conv_b)
    a, g = jnp.split(up, 2, axis=-1)
    y = (jax.nn.gelu(a) * g) @ w_down
    x = x + gate_f * rms_norm(y, g_post_ffn)
    return x


def setup_inputs(seed: int = 0) -> dict:
    key = jax.random.key(seed)
    ks = jax.random.split(key, 24)
    f32 = jnp.float32
    L, D = DEPTH, D_MODEL

    def nrm(k, shape, scale):
        return jax.random.normal(k, shape, f32) * scale

    def gain(k, shape):
        return 1.0 + 0.02 * jax.random.normal(k, shape, f32)

    return {
        "x": jax.random.normal(ks[0], (BATCH, SEQ, D), f32),
        "c": jax.random.normal(ks[1], (BATCH, D), f32),
        "w_ada": nrm(ks[2], (L, D, N_MOD * D), 0.5 * D ** -0.5),
        "b_ada": nrm(ks[3], (L, N_MOD * D), 0.01),
        "g_pre_mix": gain(ks[4], (L, D)),
        "g_post_mix": gain(ks[5], (L, D)),
        "w_in": nrm(ks[6], (L, D, IN_COLS), D ** -0.5),
        "ln_v_g": gain(ks[7], (L, A_WIDTH)),
        "ln_v_b": nrm(ks[8], (L, A_WIDTH), 0.01),
        "w_spatial": nrm(ks[9], (L, A_HEADS, CHUNK, CHUNK), CHUNK ** -0.5),
        "b_spatial": gain(ks[10], (L, A_HEADS, CHUNK)),
        "w_pool": nrm(ks[11], (L, B_GROUPS, B_GROUP_DIM, B_GROUP_DIM), B_GROUP_DIM ** -0.5),
        "pool_scale": gain(ks[12], (L, B_WIDTH)),
        "w_branch_a": nrm(ks[13], (L, A_WIDTH, D), A_WIDTH ** -0.5),
        "w_branch_b": nrm(ks[14], (L, B_WIDTH, D), B_WIDTH ** -0.5),
        "w_out": nrm(ks[15], (L, D, D), D ** -0.5),
        "g_pre_ffn": gain(ks[16], (L, D)),
        "g_post_ffn": gain(ks[17], (L, D)),
        "w_up": nrm(ks[18], (L, D, 2 * FFN_DIM), D ** -0.5),
        "conv_w": nrm(ks[19], (L, CONV_WIDTH, 2 * FFN_DIM), CONV_WIDTH ** -0.5),
        "conv_b": nrm(ks[20], (L, 2 * FFN_DIM), 0.01),
        "w_down": nrm(ks[21], (L, FFN_DIM, D), FFN_DIM ** -0.5),
    }


def reference(x, c, w_ada, b_ada, g_pre_mix, g_post_mix, w_in, ln_v_g, ln_v_b,
              w_spatial, b_spatial, w_pool, pool_scale, w_branch_a, w_branch_b,
              w_out, g_pre_ffn, g_post_ffn, w_up, conv_w, conv_b, w_down):
    for l in range(DEPTH):
        x = hybrid_layer(x, c, w_ada[l], b_ada[l], g_pre_mix[l], g_post_mix[l], w_in[l],
                         ln_v_g[l], ln_v_b[l], w_spatial[l], b_spatial[l], w_pool[l],
                         pool_scale[l], w_branch_a[l], w_branch_b[l], w_out[l],
                         g_pre_ffn[l], g_post_ffn[l], w_up[l], conv_w[l], conv_b[l], w_down[l])
    return x
```

```python
import functools
import math

import jax
import jax.numpy as jnp
from jax import lax
from jax.experimental import pallas as pl
from jax.experimental.pallas import tpu as pltpu

D = 2048
SEQ = 8192
CHUNK = 128
HEADS = 8
HEAD_DIM = D // HEADS
A_WIDTH = D
POOL_WINDOWS = (2, 4, 8, 16)
GROUP_DIM = 256
B_WIDTH = len(POOL_WINDOWS) * GROUP_DIM
IN_COLS = 2 * A_WIDTH + B_WIDTH + 2 * D
FFN = 5632
EPS = 1e-6

F32 = jnp.float32
BF16 = jnp.bfloat16

SHIFT_M, SCALE_M, GATE_M, SHIFT_F, SCALE_F, GATE_F = range(6)

GELU_C0 = math.sqrt(2.0 / math.pi)
GELU_C1 = 0.044715

VMEM_LIMIT = 56 * 1024 * 1024

U_BLK, V_BLK, GA_BLK, GB_BLK = 0, 1, 2, 3
P_BLK = 8

POOL_HALO = 32
CONV_HALO = 16


def _mod_spec(slot, grid_rank):
    if grid_rank == 1:
        return pl.BlockSpec((1, D), lambda i: (0, slot))
    return pl.BlockSpec((1, D), lambda i, j: (0, slot))


def _row_spec(grid_rank, width=D):
    if grid_rank == 1:
        return pl.BlockSpec((1, width), lambda i: (0, 0))
    return pl.BlockSpec((1, width), lambda i, j: (0, 0))


def _resident(shape):
    zeros = (0,) * len(shape)
    return pl.BlockSpec(shape, lambda i: zeros, pipeline_mode=pl.Buffered(1))


ADA_TN = 1024


def _ada_kernel(c_ref, w_ref, b_ref, o_ref):
    c = c_ref[...]
    s = c * jax.nn.sigmoid(c)
    o_ref[...] = jnp.sum(w_ref[...] * s, axis=0, keepdims=True) + b_ref[...]


def _ada_mod(c_col, w_ada, b_ada):
    n = w_ada.shape[1]
    return pl.pallas_call(
        _ada_kernel,
        out_shape=jax.ShapeDtypeStruct((1, n), F32),
        grid=(n // ADA_TN,),
        in_specs=[
            pl.BlockSpec((D, 1), lambda j: (0, 0)),
            pl.BlockSpec((D, ADA_TN), lambda j: (0, j)),
            pl.BlockSpec((1, ADA_TN), lambda j: (0, j)),
        ],
        out_specs=pl.BlockSpec((1, ADA_TN), lambda j: (0, j)),
        compiler_params=pltpu.CompilerParams(
            dimension_semantics=("arbitrary",), vmem_limit_bytes=VMEM_LIMIT),
        name="ada_mod",
    )(c_col, w_ada, b_ada)


def _modulated_rms(x, g, scale, shift):
    ms = jnp.mean(x * x, axis=-1, keepdims=True)
    y = x * lax.rsqrt(ms + EPS)
    return y * g * (1.0 + scale) + shift


PROJ_TM = 1024
PROJ_TN = 1024
NORM_ROWS = 256


def _proj_kernel(x_ref, shift_ref, scale_ref, g_ref, w_ref, o_ref, h_ref):
    j = pl.program_id(1)

    @pl.when(j == 0)
    def _():
        def body(r, carry):
            rows = pl.ds(pl.multiple_of(r * NORM_ROWS, NORM_ROWS), NORM_ROWS)
            h = _modulated_rms(x_ref[rows, :], g_ref[...], scale_ref[...], shift_ref[...])
            h_ref[rows, :] = h.astype(BF16)
            return carry
        lax.fori_loop(0, PROJ_TM // NORM_ROWS, body, 0)

    acc = jnp.dot(h_ref[...], w_ref[...], preferred_element_type=F32)

    n_gelu = (2 * A_WIDTH) // PROJ_TN
    n_sig = (2 * D) // PROJ_TN
    is_gelu = j < n_gelu
    is_sig = jnp.logical_and(j >= n_gelu, j < n_gelu + n_sig)
    a = jnp.where(is_gelu, GELU_C0, jnp.where(is_sig, 0.5, 0.0)).astype(F32)
    b = jnp.where(is_gelu, GELU_C0 * GELU_C1, 0.0).astype(F32)
    alpha = jnp.where(is_sig, 0.5, 0.0).astype(F32)
    beta = jnp.where(is_gelu, 0.5, jnp.where(is_sig, 0.0, 1.0)).astype(F32)
    t = jnp.tanh(acc * (a + b * (acc * acc)))
    o_ref[...] = ((1.0 + t) * (alpha + beta * acc)).astype(BF16)


def _proj(x, mod, g_pre, w_in_bf16):
    grid = (SEQ // PROJ_TM, IN_COLS // PROJ_TN)
    return pl.pallas_call(
        _proj_kernel,
        out_shape=jax.ShapeDtypeStruct((SEQ, IN_COLS), BF16),
        grid=grid,
        in_specs=[
            pl.BlockSpec((PROJ_TM, D), lambda i, j: (i, 0)),
            _mod_spec(SHIFT_M, 2),
            _mod_spec(SCALE_M, 2),
            _row_spec(2),
            pl.BlockSpec((D, PROJ_TN), lambda i, j: (0, j)),
        ],
        out_specs=pl.BlockSpec((PROJ_TM, PROJ_TN), lambda i, j: (i, j)),
        scratch_shapes=[pltpu.VMEM((PROJ_TM, D), BF16)],
        compiler_params=pltpu.CompilerParams(
            dimension_semantics=("arbitrary", "arbitrary"), vmem_limit_bytes=VMEM_LIMIT),
        name="in_proj",
    )(x, mod, mod, g_pre, w_in_bf16)


MIX_TM = 256
MIX_TN = 512
MIX_L = POOL_HALO + MIX_TM


def _mix_kernel(u_ref, v_ref, ga_ref, gb_ref, p_ref, ph_ref, x_ref, gate_ref,
                lng_ref, lnb_ref, ws_ref, bs_ref, wpool_ref, pscale_ref,
                wa_ref, wb_ref, wout_ref, gpost_ref,
                o_ref,
                gated_s, pe_s, s2_s, s4_s, s8_s, s16_s, ybin_s, merged_s, mix_s):
    i = pl.program_id(0)

    def chunk_body(c, carry):
        rows = pl.ds(pl.multiple_of(c * CHUNK, CHUNK), CHUNK)
        gv = v_ref[rows, :].astype(F32)
        mu = jnp.mean(gv, axis=-1, keepdims=True)
        xc = gv - mu
        var = jnp.mean(xc * xc, axis=-1, keepdims=True)
        vn = (xc * lax.rsqrt(var + EPS) * lng_ref[...] + lnb_ref[...]).astype(BF16)
        for h in range(HEADS):
            cols = slice(h * HEAD_DIM, (h + 1) * HEAD_DIM)
            mixed = jnp.dot(ws_ref[h], vn[:, cols], preferred_element_type=F32)
            mixed = mixed + bs_ref[:, h:h + 1]
            gated_s[rows, cols] = (u_ref[rows, cols].astype(F32) * mixed).astype(BF16)
        return carry
    lax.fori_loop(0, MIX_TM // CHUNK, chunk_body, 0)

    halo_on = (i > 0).astype(F32)
    pe_s[0:POOL_HALO, :] = ph_ref[...].astype(F32) * halo_on
    pe_s[POOL_HALO:MIX_L, :] = p_ref[...].astype(F32)
    g = GROUP_DIM
    s2_s[8:MIX_L, :] = pe_s[8:MIX_L, :] + pe_s[7:MIX_L - 1, :]
    s4_s[16:MIX_L, :] = s2_s[16:MIX_L, g:] + s2_s[14:MIX_L - 2, g:]
    s8_s[24:MIX_L, :] = s4_s[24:MIX_L, g:] + s4_s[20:MIX_L - 4, g:]
    s16_s[32:MIX_L, :] = s8_s[32:MIX_L, g:] + s8_s[24:MIX_L - 8, g:]
    pos = (i * MIX_TM + 1 + lax.broadcasted_iota(jnp.int32, (MIX_TM, 1), 0)).astype(F32)
    sums = (s2_s, s4_s, s8_s, s16_s)
    for gi, win in enumerate(POOL_WINDOWS):
        cols = slice(gi * g, (gi + 1) * g)
        wsum = sums[gi][POOL_HALO:MIX_L, 0:g]
        mean = wsum / jnp.minimum(pos, float(win))
        pooled = (mean - pe_s[POOL_HALO:MIX_L, cols]).astype(BF16)
        yg = jnp.dot(pooled, wpool_ref[gi], preferred_element_type=F32)
        ybin_s[:, cols] = (yg * pscale_ref[:, cols]).astype(BF16)

    for nt in range(D // MIX_TN):
        cols = slice(nt * MIX_TN, (nt + 1) * MIX_TN)
        ya = jnp.dot(gated_s[...], wa_ref[:, cols], preferred_element_type=F32)
        yb = jnp.dot(ybin_s[...], wb_ref[:, cols], preferred_element_type=F32)
        merged = ga_ref[:, cols].astype(F32) * ya + gb_ref[:, cols].astype(F32) * yb
        merged_s[:, cols] = merged.astype(BF16)

    ssq = jnp.zeros((MIX_TM, 1), F32)
    for nt in range(D // MIX_TN):
        cols = slice(nt * MIX_TN, (nt + 1) * MIX_TN)
        mix = jnp.dot(merged_s[...], wout_ref[:, cols], preferred_element_type=F32)
        mix_s[:, cols] = mix
        ssq = ssq + jnp.sum(mix * mix, axis=-1, keepdims=True)
    inv = lax.rsqrt(ssq * (1.0 / D) + EPS)
    o_ref[...] = x_ref[...] + gate_ref[...] * (mix_s[...] * inv * gpost_ref[...])


def _mix(proj_act, x, mod, ln_g, ln_b, ws_tril, bs_t, w_pool, pool_scale, wa, wb, wout, g_post):
    halo_blocks = MIX_TM // POOL_HALO
    act = lambda blk: pl.BlockSpec((MIX_TM, D), lambda i: (i, blk))
    return pl.pallas_call(
        _mix_kernel,
        out_shape=jax.ShapeDtypeStruct((SEQ, D), F32),
        grid=(SEQ // MIX_TM,),
        in_specs=[
            act(U_BLK), act(V_BLK), act(GA_BLK), act(GB_BLK),
            pl.BlockSpec((MIX_TM, B_WIDTH), lambda i: (i, P_BLK)),
            pl.BlockSpec((POOL_HALO, B_WIDTH),
                         lambda i: (jnp.maximum(i * halo_blocks - 1, 0), P_BLK)),
            pl.BlockSpec((MIX_TM, D), lambda i: (i, 0)),
            _mod_spec(GATE_M, 1),
            _row_spec(1), _row_spec(1),
            _resident((HEADS, CHUNK, CHUNK)),
            _resident((CHUNK, HEADS)),
            _resident((len(POOL_WINDOWS), GROUP_DIM, GROUP_DIM)),
            _row_spec(1, B_WIDTH),
            _resident((A_WIDTH, D)),
            _resident((B_WIDTH, D)),
            _resident((D, D)),
            _row_spec(1),
        ],
        out_specs=pl.BlockSpec((MIX_TM, D), lambda i: (i, 0)),
        scratch_shapes=[
            pltpu.VMEM((MIX_TM, A_WIDTH), BF16),
            pltpu.VMEM((MIX_L, B_WIDTH), F32),
            pltpu.VMEM((MIX_L, B_WIDTH), F32),
            pltpu.VMEM((MIX_L, B_WIDTH - GROUP_DIM), F32),
            pltpu.VMEM((MIX_L, B_WIDTH - 2 * GROUP_DIM), F32),
            pltpu.VMEM((MIX_L, B_WIDTH - 3 * GROUP_DIM), F32),
            pltpu.VMEM((MIX_TM, B_WIDTH), BF16),
            pltpu.VMEM((MIX_TM, D), BF16),
            pltpu.VMEM((MIX_TM, D), F32),
        ],
        compiler_params=pltpu.CompilerParams(
            dimension_semantics=("arbitrary",), vmem_limit_bytes=VMEM_LIMIT),
        name="token_mix",
    )(proj_act, proj_act, proj_act, proj_act, proj_act, proj_act, x, mod,
      ln_g, ln_b, ws_tril, bs_t, w_pool, pool_scale, wa, wb, wout, g_post)


UP_TM = 1024
UP_TF = 512
UP_L = CONV_HALO + UP_TM


def _up_kernel(x_ref, xh_ref, shift_ref, scale_ref, g_ref, wa_ref, wg_ref,
               cwa_ref, cwg_ref, cba_ref, cbg_ref, o_ref, h_ref, ua_s, ug_s):
    i = pl.program_id(0)
    j = pl.program_id(1)

    @pl.when(j == 0)
    def _():
        hh = _modulated_rms(xh_ref[...], g_ref[...], scale_ref[...], shift_ref[...])
        h_ref[0:CONV_HALO, :] = hh.astype(BF16)

        def body(r, carry):
            rows = pl.ds(pl.multiple_of(r * NORM_ROWS, NORM_ROWS), NORM_ROWS)
            orow = pl.ds(pl.multiple_of(CONV_HALO + r * NORM_ROWS, CONV_HALO), NORM_ROWS)
            h = _modulated_rms(x_ref[rows, :], g_ref[...], scale_ref[...], shift_ref[...])
            h_ref[orow, :] = h.astype(BF16)
            return carry
        lax.fori_loop(0, UP_TM // NORM_ROWS, body, 0)

    halo_on = (i > 0).astype(F32)

    def conv(w_ref, cw_ref, cb_ref, s_ref):
        s_ref[...] = jnp.dot(h_ref[...], w_ref[...], preferred_element_type=F32)
        s_ref[0:CONV_HALO, :] = s_ref[0:CONV_HALO, :] * halo_on
        out = cb_ref[...] + s_ref[CONV_HALO - 2:UP_L - 2, :] * cw_ref[0:1, :]
        out = out + s_ref[CONV_HALO - 1:UP_L - 1, :] * cw_ref[1:2, :]
        out = out + s_ref[CONV_HALO:UP_L, :] * cw_ref[2:3, :]
        return out

    a = conv(wa_ref, cwa_ref, cba_ref, ua_s)
    gte = conv(wg_ref, cwg_ref, cbg_ref, ug_s)
    t = jnp.tanh(a * (GELU_C0 + (GELU_C0 * GELU_C1) * (a * a)))
    o_ref[...] = ((0.5 * a) * (1.0 + t) * gte).astype(BF16)


def _ffn_up(x1, mod, g_pre, w_up_bf16, conv_w, conv_b):
    nf = FFN // UP_TF
    halo_blocks = UP_TM // CONV_HALO
    return pl.pallas_call(
        _up_kernel,
        out_shape=jax.ShapeDtypeStruct((SEQ, FFN), BF16),
        grid=(SEQ // UP_TM, nf),
        in_specs=[
            pl.BlockSpec((UP_TM, D), lambda i, j: (i, 0)),
            pl.BlockSpec((CONV_HALO, D), lambda i, j: (jnp.maximum(i * halo_blocks - 1, 0), 0)),
            _mod_spec(SHIFT_F, 2), _mod_spec(SCALE_F, 2), _row_spec(2),
            pl.BlockSpec((D, UP_TF), lambda i, j: (0, j)),
            pl.BlockSpec((D, UP_TF), lambda i, j: (0, j + nf)),
            pl.BlockSpec((3, UP_TF), lambda i, j: (0, j)),
            pl.BlockSpec((3, UP_TF), lambda i, j: (0, j + nf)),
            pl.BlockSpec((1, UP_TF), lambda i, j: (0, j)),
            pl.BlockSpec((1, UP_TF), lambda i, j: (0, j + nf)),
        ],
        out_specs=pl.BlockSpec((UP_TM, UP_TF), lambda i, j: (i, j)),
        scratch_shapes=[
            pltpu.VMEM((UP_L, D), BF16),
            pltpu.VMEM((UP_L, UP_TF), F32),
            pltpu.VMEM((UP_L, UP_TF), F32),
        ],
        compiler_params=pltpu.CompilerParams(
            dimension_semantics=("arbitrary", "arbitrary"), vmem_limit_bytes=VMEM_LIMIT),
        name="ffn_up",
    )(x1, x1, mod, mod, g_pre, w_up_bf16, w_up_bf16, conv_w, conv_w, conv_b, conv_b)


DOWN_TM = 256
DOWN_TN = 512


def _down_kernel(a_ref, w_ref, x_ref, gate_ref, gpost_ref, o_ref, y_s):
    ssq = jnp.zeros((DOWN_TM, 1), F32)
    for nt in range(D // DOWN_TN):
        cols = slice(nt * DOWN_TN, (nt + 1) * DOWN_TN)
        y = jnp.dot(a_ref[...], w_ref[:, cols], preferred_element_type=F32)
        y_s[:, cols] = y
        ssq = ssq + jnp.sum(y * y, axis=-1, keepdims=True)
    inv = lax.rsqrt(ssq * (1.0 / D) + EPS)
    o_ref[...] = x_ref[...] + gate_ref[...] * (y_s[...] * inv * gpost_ref[...])


def _ffn_down(act, w_down_bf16, x1, mod, g_post):
    return pl.pallas_call(
        _down_kernel,
        out_shape=jax.ShapeDtypeStruct((SEQ, D), F32),
        grid=(SEQ // DOWN_TM,),
        in_specs=[
            pl.BlockSpec((DOWN_TM, FFN), lambda i: (i, 0)),
            _resident((FFN, D)),
            pl.BlockSpec((DOWN_TM, D), lambda i: (i, 0)),
            _mod_spec(GATE_F, 1),
            _row_spec(1),
        ],
        out_specs=pl.BlockSpec((DOWN_TM, D), lambda i: (i, 0)),
        scratch_shapes=[pltpu.VMEM((DOWN_TM, D), F32)],
        compiler_params=pltpu.CompilerParams(
            dimension_semantics=("arbitrary",), vmem_limit_bytes=VMEM_LIMIT),
        name="ffn_down",
    )(act, w_down_bf16, x1, mod, g_post)


def _layer(x, c, w_ada, b_ada, g_pre_mix, g_post_mix, w_in, ln_v_g, ln_v_b,
           w_spatial, b_spatial, w_pool, pool_scale, w_branch_a, w_branch_b,
           w_out, g_pre_ffn, g_post_ffn, w_up, conv_w, conv_b, w_down):
    row = lambda v: v.reshape(1, -1)
    mod = _ada_mod(c.reshape(D, 1), w_ada, row(b_ada))

    a2 = 2 * A_WIDTH
    w_in_r = jnp.concatenate(
        [w_in[:, :a2], w_in[:, a2 + B_WIDTH:], w_in[:, a2:a2 + B_WIDTH]], axis=1).astype(BF16)
    tril = jnp.tril(jnp.ones((CHUNK, CHUNK), dtype=bool))
    ws_tril = jnp.where(tril[None], w_spatial, 0.0).astype(BF16)

    proj_act = _proj(x, mod, row(g_pre_mix), w_in_r)
    x1 = _mix(proj_act, x, mod, row(ln_v_g), row(ln_v_b), ws_tril, b_spatial.T,
              w_pool.astype(BF16), row(pool_scale), w_branch_a.astype(BF16),
              w_branch_b.astype(BF16), w_out.astype(BF16), row(g_post_mix))
    act = _ffn_up(x1, mod, row(g_pre_ffn), w_up.astype(BF16), conv_w, row(conv_b))
    return _ffn_down(act, w_down.astype(BF16), x1, mod, row(g_post_ffn))


def kernel(x, c, w_ada, b_ada, g_pre_mix, g_post_mix, w_in, ln_v_g, ln_v_b, w_spatial,
           b_spatial, w_pool, pool_scale, w_branch_a, w_branch_b, w_out, g_pre_ffn,
           g_post_ffn, w_up, conv_w, conv_b, w_down):
    xs = x.reshape(SEQ, D)
    for l in range(w_ada.shape[0]):
        xs = _layer(xs, c, w_ada[l], b_ada[l], g_pre_mix[l], g_post_mix[l], w_in[l],
                    ln_v_g[l], ln_v_b[l], w_spatial[l], b_spatial[l], w_pool[l],
                    pool_scale[l], w_branch_a[l], w_branch_b[l], w_out[l], g_pre_ffn[l],
                    g_post_ffn[l], w_up[l], conv_w[l], conv_b[l], w_down[l])
    return xs.reshape(x.shape)
```
